```python
import jax, jax.numpy as jnp
from jax import lax
import numpy as np

D_MODEL = 1024
BATCH = 4
SEQ = 4096
DEPTH = 1

CHUNK = 64
MIX_WIDTH = D_MODEL
HGRN_WIDTH = MIX_WIDTH // 2
CONV_WIDTH = MIX_WIDTH - HGRN_WIDTH
HGRN_HEAD_DIM = 128
HGRN_HEADS = HGRN_WIDTH // HGRN_HEAD_DIM
CONV_K = 3
D_FF = 4 * D_MODEL
ALPHA = (2 * DEPTH) ** 0.25
BETA = (8 * DEPTH) ** -0.25
EPS = 1e-5
IN_COLS = 4 * HGRN_WIDTH + 3 * CONV_WIDTH
SPLITS = (HGRN_WIDTH, 2 * HGRN_WIDTH, 3 * HGRN_WIDTH, 4 * HGRN_WIDTH,
          4 * HGRN_WIDTH + CONV_WIDTH, 4 * HGRN_WIDTH + 2 * CONV_WIDTH)

kernel_name = "hybrid_hgrn2_shortconv_deepnorm_layer"


def layer_norm(x, g, b):
    xf = x.astype(jnp.float32)
    mu = jnp.mean(xf, axis=-1, keepdims=True)
    xc = xf - mu
    var = jnp.mean(jnp.square(xc), axis=-1, keepdims=True)
    y = xc * lax.rsqrt(var + EPS) * g.astype(jnp.float32) + b.astype(jnp.float32)
    return y.astype(x.dtype)


def hgrn2_chunkwise(q, k, v, g):
    bsz, seq, n_h, d_k = q.shape
    d_v = v.shape[-1]
    n_c = seq // CHUNK

    def to_chunks(a):
        return a.reshape(bsz, n_c, CHUNK, n_h, a.shape[-1]).transpose(1, 0, 3, 2, 4)

    q, k, v, g = to_chunks(q), to_chunks(k), to_chunks(v), to_chunks(g)
    b = jnp.cumsum(g, axis=-2)
    b_ref = b[..., CHUNK // 2:CHUNK // 2 + 1, :]
    b_last = b[..., -1:, :]
    causal = jnp.tril(jnp.ones((CHUNK, CHUNK), dtype=bool))
    scores = jnp.einsum('nbhck,nbhsk->nbhcs', q * jnp.exp(b - b_ref), k * jnp.exp(b_ref - b))
    scores = jnp.where(causal, scores, 0.0)
    o_intra = jnp.einsum('nbhcs,nbhsv->nbhcv', scores, v)
    q_inter = q * jnp.exp(b)
    k_state = k * jnp.exp(b_last - b)
    chunk_decay = jnp.exp(b_last[..., 0, :])

    def step(state, inp):
        q_c, k_c, v_c, d_c = inp
        o_c = jnp.einsum('bhck,bhkv->bhcv', q_c, state)
        state = d_c[..., None] * state + jnp.einsum('bhck,bhcv->bhkv', k_c, v_c)
        return state, o_c

    s0 = jnp.zeros((bsz, n_h, d_k, d_v), q.dtype)
    _, o_inter = lax.scan(step, s0, (q_inter, k_state, v, chunk_decay))
    o = o_intra + o_inter
    return o.transpose(1, 0, 3, 2, 4).reshape(bsz, seq, n_h, d_v)


def causal_depthwise_conv(z, w):
    rhs = w.astype(z.dtype).reshape(CONV_K, 1, z.shape[-1])
    return lax.conv_general_dilated(
        z, rhs, window_strides=(1,), padding=[(CONV_K - 1, 0)],
        dimension_numbers=('NWC', 'WIO', 'NWC'), feature_group_count=z.shape[-1])


def token_mixer(h, w_in, lower_bound, gate_norm_w, conv_w, w_out):
    bsz, seq, _ = h.shape
    proj = h @ w_in
    q, f_pre, i_in, o_gate, b_gate, c_gate, u = jnp.split(proj, SPLITS, axis=-1)

    f = lower_bound + (1.0 - lower_bound) * jax.nn.sigmoid(f_pre.astype(jnp.float32))
    log_f = jnp.log(f)
    k = 1.0 - f
    heads = lambda a: a.reshape(bsz, seq, HGRN_HEADS, HGRN_HEAD_DIM)
    o = hgrn2_chunkwise(heads(q.astype(jnp.float32)), heads(k),
                        heads(i_in.astype(jnp.float32)), heads(log_f))
    o = o * lax.rsqrt(jnp.mean(jnp.square(o), axis=-1, keepdims=True) + EPS)
    o = o.reshape(bsz, seq, HGRN_WIDTH) * gate_norm_w.astype(jnp.float32) \
        * jax.nn.silu(o_gate.astype(jnp.float32))
    o = o.astype(h.dtype)

    y = b_gate * causal_depthwise_conv(c_gate * u, conv_w)

    return jnp.concatenate([o, y], axis=-1) @ w_out


def squared_relu_mlp(h, w1, w2):
    return jnp.square(jax.nn.relu(h @ w1)) @ w2


def setup_inputs(seed: int = 0) -> dict:
    key = jax.random.key(seed)
    ks = jax.random.split(key, 12)
    nrm = lambda k, shape: jax.random.normal(k, shape, jnp.float32)
    return {
        "x": nrm(ks[0], (BATCH, SEQ, D_MODEL)),
        "w_in": nrm(ks[1], (DEPTH, D_MODEL, IN_COLS)) * D_MODEL ** -0.5,
        "lb_logits": 0.1 * nrm(ks[2], (DEPTH + 1, HGRN_WIDTH)),
        "gate_norm_w": 1.0 + 0.02 * nrm(ks[3], (DEPTH, HGRN_WIDTH)),
        "conv_w": nrm(ks[4], (DEPTH, CONV_K, CONV_WIDTH)) * CONV_K ** -0.5,
        "w_out": nrm(ks[5], (DEPTH, MIX_WIDTH, D_MODEL)) * (MIX_WIDTH ** -0.5 * BETA),
        "ln1_g": 1.0 + 0.02 * nrm(ks[6], (DEPTH, D_MODEL)),
        "ln1_b": 0.02 * nrm(ks[7], (DEPTH, D_MODEL)),
        "w_ff1": nrm(ks[8], (DEPTH, D_MODEL, D_FF)) * D_MODEL ** -0.5,
        "w_ff2": nrm(ks[9], (DEPTH, D_FF, D_MODEL)) * (D_FF ** -0.5 * BETA),
        "ln2_g": 1.0 + 0.02 * nrm(ks[10], (DEPTH, D_MODEL)),
        "ln2_b": 0.02 * nrm(ks[11], (DEPTH, D_MODEL)),
    }


def reference(x, w_in, lb_logits, gate_norm_w, conv_w, w_out, ln1_g, ln1_b,
              w_ff1, w_ff2, ln2_g, ln2_b):
    lower_bounds = jnp.cumsum(jax.nn.softmax(lb_logits.astype(jnp.float32), axis=0), axis=0)
    h = x
    for l in range(DEPTH):
        mix = token_mixer(h, w_in[l], lower_bounds[l], gate_norm_w[l], conv_w[l], w_out[l])
        h = layer_norm(ALPHA * h + mix, ln1_g[l], ln1_b[l])
        h = layer_norm(ALPHA * h + squared_relu_mlp(h, w_ff1[l], w_ff2[l]), ln2_g[l], ln2_b[l])
    return h
```

```python
import functools

import jax
import jax.numpy as jnp
from jax import lax
from jax.experimental import pallas as pl
from jax.experimental.pallas import tpu as pltpu

D_MODEL = 1024
HGRN_WIDTH = 512
CONV_WIDTH = 512
HEAD_DIM = 128
HEADS = HGRN_WIDTH // HEAD_DIM
CHUNK = 64
CONV_K = 3
D_FF = 4 * D_MODEL
IN_COLS = 4 * HGRN_WIDTH + 3 * CONV_WIDTH
DEPTH = 1
ALPHA = (2 * DEPTH) ** 0.25
EPS = 1e-5

Q0, F0, I0, G0, B0, C0, U0 = (0, 512, 1024, 1536, 2048, 2560, 3072)

TIME_TILE = 512
TOKEN_TILE = 512
FF_BLOCK = 1024
CARRY_ROWS = 8
VMEM_LIMIT_BYTES = 56 * 1024 * 1024

F32 = jnp.float32
BF16 = jnp.bfloat16


def _layer_norm(x, g, b):
    mu = jnp.mean(x, axis=-1, keepdims=True)
    xc = x - mu
    var = jnp.mean(xc * xc, axis=-1, keepdims=True)
    return xc * lax.rsqrt(var + EPS) * g + b


def _split3_bf16(a):
    hi = a.astype(BF16)
    r1 = a - hi.astype(F32)
    mid = r1.astype(BF16)
    lo = (r1 - mid.astype(F32)).astype(BF16)
    return hi, mid, lo


def _mixer_kernel(x_ref, w_in_ref, lbl_ref, gnw_ref, cw_ref, w_out_ref, g1_ref, b1_ref,
                  o_ref, proj_ref, st_ref, zbuf_ref, mix_ref):
    tt = x_ref.shape[1]
    t = pl.program_id(1)

    @pl.when(t == 0)
    def _():
        st_ref[...] = jnp.zeros_like(st_ref)
        zbuf_ref[0:CARRY_ROWS, :] = jnp.zeros((CARRY_ROWS, CONV_WIDTH), F32)

    x = x_ref[0]
    proj_ref[...] = jnp.dot(x.astype(BF16), w_in_ref[...], preferred_element_type=F32)

    lbl = lbl_ref[...]
    e = jnp.exp(lbl - jnp.max(lbl, axis=0, keepdims=True))
    lb = e[0:1, :] / jnp.sum(e, axis=0, keepdims=True)

    row = lax.broadcasted_iota(jnp.int32, (CHUNK, CHUNK), 0)
    col = lax.broadcasted_iota(jnp.int32, (CHUNK, CHUNK), 1)
    causal = row >= col
    tril = causal.astype(BF16)
    gnw = gnw_ref[...]

    def chunk_body(c, carry):
        r0 = pl.multiple_of(c * CHUNK, CHUNK)
        rows = pl.ds(r0, CHUNK)
        q = proj_ref[rows, Q0:Q0 + HGRN_WIDTH]
        f_pre = proj_ref[rows, F0:F0 + HGRN_WIDTH]
        v = proj_ref[rows, I0:I0 + HGRN_WIDTH]
        gate = proj_ref[rows, G0:G0 + HGRN_WIDTH]

        f = lb + (1.0 - lb) * jax.nn.sigmoid(f_pre)
        g = jnp.log(f)
        k = 1.0 - f
        g_hi, g_mid, g_lo = _split3_bf16(g)
        b = (jnp.dot(tril, g_hi, preferred_element_type=F32)
             + jnp.dot(tril, g_mid, preferred_element_type=F32)
             + jnp.dot(tril, g_lo, preferred_element_type=F32))
        b_ref = b[CHUNK // 2:CHUNK // 2 + 1, :]
        b_last = b[CHUNK - 1:CHUNK, :]
        qd = (q * jnp.exp(b - b_ref)).astype(BF16)
        kd = (k * jnp.exp(b_ref - b)).astype(BF16)
        qi = (q * jnp.exp(b)).astype(BF16)
        ks = (k * jnp.exp(b_last - b)).astype(BF16)
        decay = jnp.exp(b_last)
        vb = v.astype(BF16)
        silu_gate = gate * jax.nn.sigmoid(gate)

        for h in range(HEADS):
            cs = slice(h * HEAD_DIM, (h + 1) * HEAD_DIM)
            scores = lax.dot_general(qd[:, cs], kd[:, cs], (((1,), (1,)), ((), ())),
                                     preferred_element_type=F32)
            scores = jnp.where(causal, scores, 0.0).astype(BF16)
            st = st_ref[h]
            o_h = (jnp.dot(scores, vb[:, cs], preferred_element_type=F32)
                   + lax.dot_general(qi[:, cs], st.astype(BF16), (((1,), (1,)), ((), ())),
                                     preferred_element_type=F32))
            upd = lax.dot_general(vb[:, cs], ks[:, cs], (((0,), (0,)), ((), ())),
                                  preferred_element_type=F32)
            st_ref[h] = st * decay[:, cs] + upd
            o_h = o_h * lax.rsqrt(jnp.mean(o_h * o_h, axis=-1, keepdims=True) + EPS)
            o_h = o_h * gnw[:, cs] * silu_gate[:, cs]
            mix_ref[rows, cs] = o_h.astype(BF16)
        return carry

    lax.fori_loop(0, tt // CHUNK, chunk_body, 0)

    zbuf_ref[CARRY_ROWS:CARRY_ROWS + tt, :] = (proj_ref[:, C0:C0 + CONV_WIDTH]
                                               * proj_ref[:, U0:U0 + CONV_WIDTH])
    cw = cw_ref[...]
    conv = (cw[0:1, :] * zbuf_ref[CARRY_ROWS - 2:CARRY_ROWS - 2 + tt, :]
            + cw[1:2, :] * zbuf_ref[CARRY_ROWS - 1:CARRY_ROWS - 1 + tt, :]
            + cw[2:3, :] * zbuf_ref[CARRY_ROWS:CARRY_ROWS + tt, :])
    y = proj_ref[:, B0:B0 + CONV_WIDTH] * conv
    mix_ref[:, HGRN_WIDTH:] = y.astype(BF16)
    zbuf_ref[0:CARRY_ROWS, :] = zbuf_ref[tt:tt + CARRY_ROWS, :]

    mix = jnp.dot(mix_ref[...], w_out_ref[...], preferred_element_type=F32)
    o_ref[0] = _layer_norm(ALPHA * x + mix, g1_ref[...], b1_ref[...])


def _mlp_kernel(h_ref, w1_ref, w2_ref, g2_ref, b2_ref, o_ref):
    h = h_ref[...]
    hb = h.astype(BF16)
    acc = jnp.zeros(h.shape, F32)
    for j in range(D_FF // FF_BLOCK):
        cols = slice(j * FF_BLOCK, (j + 1) * FF_BLOCK)
        a = jnp.dot(hb, w1_ref[:, cols], preferred_element_type=F32)
        a = jnp.maximum(a, 0.0)
        a = (a * a).astype(BF16)
        acc = acc + jnp.dot(a, w2_ref[cols, :], preferred_element_type=F32)
    o_ref[...] = _layer_norm(ALPHA * h + acc, g2_ref[...], b2_ref[...])


def _resident(shape):
    return pl.BlockSpec(shape, lambda *_: (0,) * len(shape), pipeline_mode=pl.Buffered(1))


def _mixer(x, w_in, lb_logits, gate_norm_w, conv_w, w_out, ln_g, ln_b):
    bsz, seq, _ = x.shape
    assert seq % TIME_TILE == 0 and TIME_TILE % CHUNK == 0
    return pl.pallas_call(
        _mixer_kernel,
        grid=(bsz, seq // TIME_TILE),
        in_specs=[
            pl.BlockSpec((1, TIME_TILE, D_MODEL), lambda b, t: (b, t, 0)),
            _resident((D_MODEL, IN_COLS)),
            _resident((DEPTH + 1, HGRN_WIDTH)),
            _resident((1, HGRN_WIDTH)),
            _resident((CONV_K, CONV_WIDTH)),
            _resident((D_MODEL, D_MODEL)),
            _resident((1, D_MODEL)),
            _resident((1, D_MODEL)),
        ],
        out_specs=pl.BlockSpec((1, TIME_TILE, D_MODEL), lambda b, t: (b, t, 0)),
        out_shape=jax.ShapeDtypeStruct(x.shape, F32),
        scratch_shapes=[
            pltpu.VMEM((TIME_TILE, IN_COLS), F32),
            pltpu.VMEM((HEADS, HEAD_DIM, HEAD_DIM), F32),
            pltpu.VMEM((CARRY_ROWS + TIME_TILE, CONV_WIDTH), F32),
            pltpu.VMEM((TIME_TILE, D_MODEL), BF16),
        ],
        compiler_params=pltpu.CompilerParams(
            dimension_semantics=("arbitrary", "arbitrary"),
            vmem_limit_bytes=VMEM_LIMIT_BYTES),
        name="hgrn2_conv_mixer",
    )(x, w_in, lb_logits, gate_norm_w, conv_w, w_out, ln_g, ln_b)


def _mlp(h, w1, w2, ln_g, ln_b):
    n = h.shape[0]
    assert n % TOKEN_TILE == 0
    return pl.pallas_call(
        _mlp_kernel,
        grid=(n // TOKEN_TILE,),
        in_specs=[
            pl.BlockSpec((TOKEN_TILE, D_MODEL), lambda i: (i, 0)),
            _resident((D_MODEL, D_FF)),
            _resident((D_FF, D_MODEL)),
            _resident((1, D_MODEL)),
            _resident((1, D_MODEL)),
        ],
        out_specs=pl.BlockSpec((TOKEN_TILE, D_MODEL), lambda i: (i, 0)),
        out_shape=jax.ShapeDtypeStruct(h.shape, F32),
        compiler_params=pltpu.CompilerParams(
            dimension_semantics=("arbitrary",),
            vmem_limit_bytes=VMEM_LIMIT_BYTES),
        name="relu2_mlp",
    )(h, w1, w2, ln_g, ln_b)


def kernel(x, w_in, lb_logits, gate_norm_w, conv_w, w_out, ln1_g, ln1_b, w_ff1, w_ff2, ln2_g, ln2_b):
    bsz, seq, d = x.shape
    h = _mixer(x, w_in[0].astype(BF16), lb_logits, gate_norm_w, conv_w[0],
               w_out[0].astype(BF16), ln1_g, ln1_b)
    h = _mlp(h.reshape(bsz * seq, d), w_ff1[0].astype(BF16), w_ff2[0].astype(BF16), ln2_g, ln2_b)
    return h.reshape(bsz, seq, d)
```

```python
import jax
import jax.numpy as jnp
from jax import lax
from jax.experimental import pallas as pl
from jax.experimental.pallas import tpu as pltpu

D_MODEL = 1024
HGRN_WIDTH = 512
CONV_WIDTH = 512
HEAD_DIM = 128
HEADS = HGRN_WIDTH // HEAD_DIM
CHUNK = 64
CONV_K = 3
D_FF = 4 * D_MODEL
IN_COLS = 4 * HGRN_WIDTH + 3 * CONV_WIDTH
DEPTH = 1
ALPHA = (2 * DEPTH) ** 0.25
EPS = 1e-5

Q0, F0, I0, G0, B0, C0, U0 = (0, 512, 1024, 1536, 2048, 2560, 3072)
SECTION = 512

TIME_TILE = 512
TOKEN_TILE = 512
FF_BLOCK = 1024
CARRY_ROWS = 8
VMEM_LIMIT_BYTES = 56 * 1024 * 1024

F32 = jnp.float32
BF16 = jnp.bfloat16

_NT = (((1,), (1,)), ((), ()))
_TN = (((0,), (0,)), ((), ()))


def _layer_norm(x, g, b):
    mu = jnp.mean(x, axis=-1, keepdims=True)
    xc = x - mu
    var = jnp.mean(xc * xc, axis=-1, keepdims=True)
    return xc * lax.rsqrt(var + EPS) * g + b


def _split3_bf16(a):
    hi = a.astype(BF16)
    r1 = a - hi.astype(F32)
    mid = r1.astype(BF16)
    lo = (r1 - mid.astype(F32)).astype(BF16)
    return hi, mid, lo


def _mixer_kernel(x_ref, w_in_ref, lbl_ref, gnw_ref, cw_ref, w_out_ref, g1_ref, b1_ref,
                  o_ref, proj_ref, st_ref, zbuf_ref, mix_ref):
    tt = x_ref.shape[1]
    n_chunks = tt // CHUNK
    t = pl.program_id(1)

    @pl.when(t == 0)
    def _():
        st_ref[...] = jnp.zeros_like(st_ref)
        zbuf_ref[0:CARRY_ROWS, :] = jnp.zeros((CARRY_ROWS, CONV_WIDTH), F32)

    x = x_ref[0]
    xb = x.astype(BF16)

    def project(c0):
        proj_ref[:, c0:c0 + SECTION] = jnp.dot(xb, w_in_ref[:, c0:c0 + SECTION],
                                               preferred_element_type=F32)

    def chunk_rows(c):
        return slice(c * CHUNK, (c + 1) * CHUNK)

    def head_cols(h):
        return slice(h * HEAD_DIM, (h + 1) * HEAD_DIM)

    lbl = lbl_ref[...]
    e = jnp.exp(lbl - jnp.max(lbl, axis=0, keepdims=True))
    lb = e[0:1, :] / jnp.sum(e, axis=0, keepdims=True)

    row = lax.broadcasted_iota(jnp.int32, (CHUNK, CHUNK), 0)
    col = lax.broadcasted_iota(jnp.int32, (CHUNK, CHUNK), 1)
    causal = row >= col
    tril = causal.astype(BF16)
    tril3 = jnp.concatenate([tril, tril, tril], axis=1)
    gnw = gnw_ref[...]

    project(F0)
    project(Q0)

    k_all, b_all = [], []
    for c in range(n_chunks):
        f = lb + (1.0 - lb) * jax.nn.sigmoid(proj_ref[chunk_rows(c), F0:F0 + SECTION])
        g = jnp.log(f)
        k_all.append(1.0 - f)
        g3 = jnp.concatenate(_split3_bf16(g), axis=0)
        b_all.append(jnp.dot(tril3, g3, preferred_element_type=F32))

    project(I0)

    scores_all, upd_all, qi_all, vb_all, decay_all = [], [], [], [], []
    for c in range(n_chunks):
        b, k = b_all[c], k_all[c]
        q = proj_ref[chunk_rows(c), Q0:Q0 + SECTION]
        b_mid = b[CHUNK // 2:CHUNK // 2 + 1, :]
        b_last = b[CHUNK - 1:CHUNK, :]
        qd = (q * jnp.exp(b - b_mid)).astype(BF16)
        kd = (k * jnp.exp(b_mid - b)).astype(BF16)
        ks = (k * jnp.exp(b_last - b)).astype(BF16)
        qi_all.append((q * jnp.exp(b)).astype(BF16))
        decay_all.append(jnp.exp(b_last))
        vb = proj_ref[chunk_rows(c), I0:I0 + SECTION].astype(BF16)
        vb_all.append(vb)
        for h in range(HEADS):
            cs = head_cols(h)
            scores_all.append(lax.dot_general(qd[:, cs], kd[:, cs], _NT,
                                              preferred_element_type=F32))
            upd_all.append(lax.dot_general(vb[:, cs], ks[:, cs], _TN,
                                           preferred_element_type=F32))

    project(G0)
    project(B0)
    project(C0)
    project(U0)

    zbuf_ref[CARRY_ROWS:CARRY_ROWS + tt, :] = (proj_ref[:, C0:C0 + SECTION]
                                               * proj_ref[:, U0:U0 + SECTION])
    cw = cw_ref[...]
    conv = (cw[0:1, :] * zbuf_ref[CARRY_ROWS - 2:CARRY_ROWS - 2 + tt, :]
            + cw[1:2, :] * zbuf_ref[CARRY_ROWS - 1:CARRY_ROWS - 1 + tt, :]
            + cw[2:3, :] * zbuf_ref[CARRY_ROWS:CARRY_ROWS + tt, :])
    mix_ref[:, HGRN_WIDTH:] = (proj_ref[:, B0:B0 + SECTION] * conv).astype(BF16)
    zbuf_ref[0:CARRY_ROWS, :] = zbuf_ref[tt:tt + CARRY_ROWS, :]

    for h in range(HEADS):
        cs = head_cols(h)
        st = st_ref[h]
        for c in range(n_chunks):
            i = c * HEADS + h
            scores = jnp.where(causal, scores_all[i], 0.0).astype(BF16)
            o_h = (jnp.dot(scores, vb_all[c][:, cs], preferred_element_type=F32)
                   + lax.dot_general(qi_all[c][:, cs], st.astype(BF16), _NT,
                                     preferred_element_type=F32))
            st = st * decay_all[c][:, cs] + upd_all[i]
            gate = proj_ref[chunk_rows(c), G0 + h * HEAD_DIM:G0 + (h + 1) * HEAD_DIM]
            o_h = o_h * lax.rsqrt(jnp.mean(o_h * o_h, axis=-1, keepdims=True) + EPS)
            o_h = o_h * gnw[:, cs] * (gate * jax.nn.sigmoid(gate))
            mix_ref[chunk_rows(c), cs] = o_h.astype(BF16)
        st_ref[h] = st

    mix = jnp.dot(mix_ref[...], w_out_ref[...], preferred_element_type=F32)
    o_ref[0] = _layer_norm(ALPHA * x + mix, g1_ref[...], b1_ref[...])


def _mlp_kernel(h_ref, w1_ref, w2_ref, g2_ref, b2_ref, o_ref):
    h = h_ref[...]
    hb = h.astype(BF16)
    acc = jnp.zeros(h.shape, F32)
    for j in range(D_FF // FF_BLOCK):
        cols = slice(j * FF_BLOCK, (j + 1) * FF_BLOCK)
        a = jnp.dot(hb, w1_ref[:, cols], preferred_element_type=F32)
        a = jnp.maximum(a, 0.0)
        a = (a * a).astype(BF16)
        acc = acc + jnp.dot(a, w2_ref[cols, :], preferred_element_type=F32)
    o_ref[...] = _layer_norm(ALPHA * h + acc, g2_ref[...], b2_ref[...])


def _resident(shape):
    return pl.BlockSpec(shape, lambda *_: (0,) * len(shape), pipeline_mode=pl.Buffered(1))


def _mixer(x, w_in, lb_logits, gate_norm_w, conv_w, w_out, ln_g, ln_b):
    bsz, seq, _ = x.shape
    assert seq % TIME_TILE == 0 and TIME_TILE % CHUNK == 0
    return pl.pallas_call(
        _mixer_kernel,
        grid=(bsz, seq // TIME_TILE),
        in_specs=[
            pl.BlockSpec((1, TIME_TILE, D_MODEL), lambda b, t: (b, t, 0)),
            _resident((D_MODEL, IN_COLS)),
            _resident((DEPTH + 1, HGRN_WIDTH)),
            _resident((1, HGRN_WIDTH)),
            _resident((CONV_K, CONV_WIDTH)),
            _resident((D_MODEL, D_MODEL)),
            _resident((1, D_MODEL)),
            _resident((1, D_MODEL)),
        ],
        out_specs=pl.BlockSpec((1, TIME_TILE, D_MODEL), lambda b, t: (b, t, 0)),
        out_shape=jax.ShapeDtypeStruct(x.shape, F32),
        scratch_shapes=[
            pltpu.VMEM((TIME_TILE, IN_COLS), F32),
            pltpu.VMEM((HEADS, HEAD_DIM, HEAD_DIM), F32),
            pltpu.VMEM((CARRY_ROWS + TIME_TILE, CONV_WIDTH), F32),
            pltpu.VMEM((TIME_TILE, D_MODEL), BF16),
        ],
        compiler_params=pltpu.CompilerParams(
            dimension_semantics=("arbitrary", "arbitrary"),
            vmem_limit_bytes=VMEM_LIMIT_BYTES),
        name="hgrn2_conv_mixer",
    )(x, w_in, lb_logits, gate_norm_w, conv_w, w_out, ln_g, ln_b)


def _mlp(h, w1, w2, ln_g, ln_b):
    n = h.shape[0]
    assert n % TOKEN_TILE == 0
    return pl.pallas_call(
        _mlp_kernel,
        grid=(n // TOKEN_TILE,),
        in_specs=[
            pl.BlockSpec((TOKEN_TILE, D_MODEL), lambda i: (i, 0)),
            _resident((D_MODEL, D_FF)),
            _resident((D_FF, D_MODEL)),
            _resident((1, D_MODEL)),
            _resident((1, D_MODEL)),
        ],
        out_specs=pl.BlockSpec((TOKEN_TILE, D_MODEL), lambda i: (i, 0)),
        out_shape=jax.ShapeDtypeStruct(h.shape, F32),
        compiler_params=pltpu.CompilerParams(
            dimension_semantics=("arbitrary",),
            vmem_limit_bytes=VMEM_LIMIT_BYTES),
        name="relu2_mlp",
    )(h, w1, w2, ln_g, ln_b)


def kernel(x, w_in, lb_logits, gate_norm_w, conv_w, w_out, ln1_g, ln1_b, w_ff1, w_ff2, ln2_g, ln2_b):
    bsz, seq, d = x.shape
    h = _mixer(x, w_in[0].astype(BF16), lb_logits, gate_norm_w, conv_w[0],
               w_out[0].astype(BF16), ln1_g, ln1_b)
    h = _mlp(h.reshape(bsz * seq, d), w_ff1[0].astype(BF16), w_ff2[0].astype(BF16), ln2_g, ln2_b)
    return h.reshape(bsz, seq, d)
```

```python
import functools

import jax
import jax.numpy as jnp
from jax import lax
from jax.experimental import pallas as pl
from jax.experimental.pallas import tpu as pltpu

D_MODEL = 1024
HGRN_WIDTH = 512
CONV_WIDTH = 512
HEAD_DIM = 128
HEADS = HGRN_WIDTH // HEAD_DIM
CHUNK = 64
CONV_K = 3
D_FF = 4 * D_MODEL
IN_COLS = 4 * HGRN_WIDTH + 3 * CONV_WIDTH
DEPTH = 1
ALPHA = (2 * DEPTH) ** 0.25
EPS = 1e-5

Q0, F0, I0, G0, B0, C0, U0 = (0, 512, 1024, 1536, 2048, 2560, 3072)
SECTION = 512

TIME_TILE = 256
CHUNKS_PER_TILE = TIME_TILE // CHUNK
FF_BLOCK = 1024
FF_BLOCKS = D_FF // FF_BLOCK
CARRY_ROWS = 8
VMEM_LIMIT_BYTES = 56 * 1024 * 1024

F32 = jnp.float32
BF16 = jnp.bfloat16

_NT = (((1,), (1,)), ((), ()))
_TN = (((0,), (0,)), ((), ()))


def _layer_norm(x, g, b):
    mu = jnp.mean(x, axis=-1, keepdims=True)
    xc = x - mu
    var = jnp.mean(xc * xc, axis=-1, keepdims=True)
    return xc * lax.rsqrt(var + EPS) * g + b


def _split3_bf16(a):
    hi = a.astype(BF16)
    r1 = a - hi.astype(F32)
    mid = r1.astype(BF16)
    lo = (r1 - mid.astype(F32)).astype(BF16)
    return hi, mid, lo


def _chunk_rows(c):
    return slice(c * CHUNK, (c + 1) * CHUNK)


def _head_cols(h):
    return slice(h * HEAD_DIM, (h + 1) * HEAD_DIM)


def _layer_kernel(x_ref, xn_ref, w_in_ref, lbl_ref, gnw_ref, cw_ref, w_out_ref, g1_ref, b1_ref,
                  w1_ref, w2_ref, g2_ref, b2_ref,
                  o_ref,
                  pq_ref, pf_ref, pi_ref, pg_ref, pb_ref, pc_ref, pu_ref,
                  st_ref, zbuf_ref, oh_ref, y_ref, h1_ref, hb_ref, acc_ref,
                  *, tiles_per_seq):
    i = pl.program_id(0)
    sec = {Q0: pq_ref, F0: pf_ref, I0: pi_ref, G0: pg_ref, B0: pb_ref, C0: pc_ref, U0: pu_ref}

    def project(xb, c0):
        sec[c0][...] = jnp.dot(xb, w_in_ref[:, c0:c0 + SECTION], preferred_element_type=F32)

    @pl.when(i == 0)
    def _():
        h1_ref[...] = jnp.zeros_like(h1_ref)
        hb_ref[...] = jnp.zeros_like(hb_ref)
        xb0 = x_ref[0].astype(BF16)
        project(xb0, F0)
        project(xb0, Q0)

    @pl.when(lax.rem(i, tiles_per_seq) == 0)
    def _():
        st_ref[...] = jnp.zeros_like(st_ref)
        zbuf_ref[0:CARRY_ROWS, :] = jnp.zeros((CARRY_ROWS, CONV_WIDTH), F32)

    x = x_ref[0]
    xb = x.astype(BF16)

    lbl = lbl_ref[...]
    e = jnp.exp(lbl - jnp.max(lbl, axis=0, keepdims=True))
    lb = e[0:1, :] / jnp.sum(e, axis=0, keepdims=True)

    row = lax.broadcasted_iota(jnp.int32, (CHUNK, CHUNK), 0)
    col = lax.broadcasted_iota(jnp.int32, (CHUNK, CHUNK), 1)
    causal = row >= col
    tril = causal.astype(BF16)
    tril3 = jnp.concatenate([tril, tril, tril], axis=1)
    gnw = gnw_ref[...]
    cw = cw_ref[...]

    k_all, b_all = {}, {}
    scores_all, upd_all, qi_all, vb_all, decay_all = {}, {}, {}, {}, {}
    state = [st_ref[h] for h in range(HEADS)]
    hidden = {}

    def cumsum_level(c):
        f = lb + (1.0 - lb) * jax.nn.sigmoid(pf_ref[_chunk_rows(c), :])
        g = jnp.log(f)
        k_all[c] = 1.0 - f
        g3 = jnp.concatenate(_split3_bf16(g), axis=0)
        b_all[c] = jnp.dot(tril3, g3, preferred_element_type=F32)

    def scores_level(c):
        b, k = b_all[c], k_all[c]
        q = pq_ref[_chunk_rows(c), :]
        b_mid = b[CHUNK // 2:CHUNK // 2 + 1, :]
        b_last = b[CHUNK - 1:CHUNK, :]
        qd = (q * jnp.exp(b - b_mid)).astype(BF16)
        kd = (k * jnp.exp(b_mid - b)).astype(BF16)
        ks = (k * jnp.exp(b_last - b)).astype(BF16)
        qi_all[c] = (q * jnp.exp(b)).astype(BF16)
        decay_all[c] = jnp.exp(b_last)
        vb = pi_ref[_chunk_rows(c), :].astype(BF16)
        vb_all[c] = vb
        for h in range(HEADS):
            cs = _head_cols(h)
            scores_all[c, h] = lax.dot_general(qd[:, cs], kd[:, cs], _NT,
                                               preferred_element_type=F32)
            upd_all[c, h] = lax.dot_general(vb[:, cs], ks[:, cs], _TN,
                                            preferred_element_type=F32)

    def output_level(c):
        for h in range(HEADS):
            cs = _head_cols(h)
            scores = jnp.where(causal, scores_all[c, h], 0.0).astype(BF16)
            o_h = (jnp.dot(scores, vb_all[c][:, cs], preferred_element_type=F32)
                   + lax.dot_general(qi_all[c][:, cs], state[h].astype(BF16), _NT,
                                     preferred_element_type=F32))
            state[h] = state[h] * decay_all[c][:, cs] + upd_all[c, h]
            gate = pg_ref[_chunk_rows(c), cs]
            o_h = o_h * lax.rsqrt(jnp.mean(o_h * o_h, axis=-1, keepdims=True) + EPS)
            o_h = o_h * gnw[:, cs] * (gate * jax.nn.sigmoid(gate))
            oh_ref[_chunk_rows(c), cs] = o_h.astype(BF16)

    def conv():
        z0 = CARRY_ROWS
        zbuf_ref[z0:z0 + TIME_TILE, :] = pc_ref[...] * pu_ref[...]
        acc = (cw[0:1, :] * zbuf_ref[z0 - 2:z0 - 2 + TIME_TILE, :]
               + cw[1:2, :] * zbuf_ref[z0 - 1:z0 - 1 + TIME_TILE, :]
               + cw[2:3, :] * zbuf_ref[z0:z0 + TIME_TILE, :])
        y_ref[...] = (pb_ref[...] * acc).astype(BF16)
        zbuf_ref[0:CARRY_ROWS, :] = zbuf_ref[TIME_TILE:TIME_TILE + CARRY_ROWS, :]

    def mixer_finish():
        mix = (jnp.dot(oh_ref[...], w_out_ref[0:HGRN_WIDTH, :], preferred_element_type=F32)
               + jnp.dot(y_ref[...], w_out_ref[HGRN_WIDTH:, :], preferred_element_type=F32))
        return _layer_norm(ALPHA * x + mix, g1_ref[...], b1_ref[...])

    def ff1(j):
        a = jnp.dot(hb_ref[...], w1_ref[:, j * FF_BLOCK:(j + 1) * FF_BLOCK],
                    preferred_element_type=F32)
        a = jnp.maximum(a, 0.0)
        hidden[j] = (a * a).astype(BF16)

    def ff2(j):
        part = jnp.dot(hidden[j], w2_ref[j * FF_BLOCK:(j + 1) * FF_BLOCK, :],
                       preferred_element_type=F32)
        if j == 0:
            acc_ref[...] = part
        else:
            acc_ref[...] = acc_ref[...] + part

    def ff2_last_and_finish():
        j = FF_BLOCKS - 1
        part = jnp.dot(hidden[j], w2_ref[j * FF_BLOCK:(j + 1) * FF_BLOCK, :],
                       preferred_element_type=F32)
        o_ref[0] = _layer_norm(ALPHA * h1_ref[...] + (acc_ref[...] + part),
                               g2_ref[...], b2_ref[...])

    ff1(0)
    for c in range(CHUNKS_PER_TILE):
        cumsum_level(c)
    project(xb, I0)
    ff2(0)
    scores_level(0)
    scores_level(1)
    project(xb, G0)
    scores_level(2)
    scores_level(3)
    ff1(1)
    output_level(0)
    output_level(1)
    project(xb, B0)
    output_level(2)
    output_level(3)
    for h in range(HEADS):
        st_ref[h] = state[h]
    ff2(1)
    project(xb, C0)
    project(xb, U0)
    ff1(2)
    conv()
    ff1(3)
    h1_new = mixer_finish()
    ff2(2)
    ff2_last_and_finish()
    xnb = xn_ref[0].astype(BF16)
    project(xnb, F0)
    project(xnb, Q0)
    h1_ref[...] = h1_new
    hb_ref[...] = h1_new.astype(BF16)


def _resident(shape):
    return pl.BlockSpec(shape, lambda *_: (0,) * len(shape), pipeline_mode=pl.Buffered(1))


def kernel(x, w_in, lb_logits, gate_norm_w, conv_w, w_out, ln1_g, ln1_b, w_ff1, w_ff2, ln2_g, ln2_b):
    bsz, seq, d = x.shape
    assert d == D_MODEL and seq % TIME_TILE == 0 and CHUNKS_PER_TILE == 4
    tiles_per_seq = seq // TIME_TILE
    n_tiles = bsz * tiles_per_seq

    def tile_index(j):
        return (j // tiles_per_seq, j % tiles_per_seq, 0)

    return pl.pallas_call(
        functools.partial(_layer_kernel, tiles_per_seq=tiles_per_seq),
        grid=(n_tiles + 1,),
        in_specs=[
            pl.BlockSpec((1, TIME_TILE, D_MODEL), lambda i: tile_index(jnp.minimum(i, n_tiles - 1))),
            pl.BlockSpec((1, TIME_TILE, D_MODEL), lambda i: tile_index(jnp.minimum(i + 1, n_tiles - 1))),
            _resident((D_MODEL, IN_COLS)),
            _resident((DEPTH + 1, HGRN_WIDTH)),
            _resident((1, HGRN_WIDTH)),
            _resident((CONV_K, CONV_WIDTH)),
            _resident((D_MODEL, D_MODEL)),
            _resident((1, D_MODEL)),
            _resident((1, D_MODEL)),
            _resident((D_MODEL, D_FF)),
            _resident((D_FF, D_MODEL)),
            _resident((1, D_MODEL)),
            _resident((1, D_MODEL)),
        ],
        out_specs=pl.BlockSpec((1, TIME_TILE, D_MODEL), lambda i: tile_index(jnp.maximum(i - 1, 0))),
        out_shape=jax.ShapeDtypeStruct(x.shape, F32),
        scratch_shapes=[
            *[pltpu.VMEM((TIME_TILE, SECTION), F32)] * 7,
            pltpu.VMEM((HEADS, HEAD_DIM, HEAD_DIM), F32),
            pltpu.VMEM((CARRY_ROWS + TIME_TILE, CONV_WIDTH), F32),
            pltpu.VMEM((TIME_TILE, HGRN_WIDTH), BF16),
            pltpu.VMEM((TIME_TILE, CONV_WIDTH), BF16),
            pltpu.VMEM((TIME_TILE, D_MODEL), F32),
            pltpu.VMEM((TIME_TILE, D_MODEL), BF16),
            pltpu.VMEM((TIME_TILE, D_MODEL), F32),
        ],
        compiler_params=pltpu.CompilerParams(
            dimension_semantics=("arbitrary",),
            vmem_limit_bytes=VMEM_LIMIT_BYTES),
        name="hgrn2_conv_mlp_layer",
    )(x, x, w_in[0].astype(BF16), lb_logits, gate_norm_w, conv_w[0], w_out[0].astype(BF16),
      ln1_g, ln1_b, w_ff1[0].astype(BF16), w_ff2[0].astype(BF16), ln2_g, ln2_b)
```

```python
import functools

import jax
import jax.numpy as jnp
from jax import lax
from jax.experimental import pallas as pl
from jax.experimental.pallas import tpu as pltpu

D_MODEL = 1024
HGRN_WIDTH = 512
CONV_WIDTH = 512
HEAD_DIM = 128
HEADS = HGRN_WIDTH // HEAD_DIM
CHUNK = 64
CONV_K = 3
D_FF = 4 * D_MODEL
IN_COLS = 4 * HGRN_WIDTH + 3 * CONV_WIDTH
DEPTH = 1
ALPHA = (2 * DEPTH) ** 0.25
EPS = 1e-5

Q0, F0, I0, G0, B0, C0, U0 = (0, 512, 1024, 1536, 2048, 2560, 3072)
SECTION = 512

TIME_TILE = 512
CHUNKS_PER_TILE = TIME_TILE // CHUNK
FF_BLOCK = 1024
FF_BLOCKS = D_FF // FF_BLOCK
CARRY_ROWS = 8
VMEM_LIMIT_BYTES = 60 * 1024 * 1024

F32 = jnp.float32
BF16 = jnp.bfloat16

_NT = (((1,), (1,)), ((), ()))
_TN = (((0,), (0,)), ((), ()))


def _layer_norm(x, g, b):
    mu = jnp.mean(x, axis=-1, keepdims=True)
    xc = x - mu
    var = jnp.mean(xc * xc, axis=-1, keepdims=True)
    return xc * lax.rsqrt(var + EPS) * g + b


def _split3_bf16(a):
    hi = a.astype(BF16)
    r1 = a - hi.astype(F32)
    mid = r1.astype(BF16)
    lo = (r1 - mid.astype(F32)).astype(BF16)
    return hi, mid, lo


def _chunk_rows(c):
    return slice(c * CHUNK, (c + 1) * CHUNK)


def _head_cols(h):
    return slice(h * HEAD_DIM, (h + 1) * HEAD_DIM)


def _layer_kernel(x_ref, xn_ref, w_in_ref, lbl_ref, gnw_ref, cw_ref, w_out_ref, g1_ref, b1_ref,
                  w1_ref, w2_ref, g2_ref, b2_ref,
                  o_ref,
                  pq_ref, pf_ref, pi_ref, pg_ref, pb_ref, pc_ref, pu_ref,
                  st_ref, zbuf_ref, oh_ref, y_ref, h1_ref, hb_ref, acc_ref,
                  *, tiles_per_seq):
    i = pl.program_id(0)
    sec = {Q0: pq_ref, F0: pf_ref, I0: pi_ref, G0: pg_ref, B0: pb_ref, C0: pc_ref, U0: pu_ref}

    def project(xb, c0):
        sec[c0][...] = jnp.dot(xb, w_in_ref[:, c0:c0 + SECTION], preferred_element_type=F32)

    @pl.when(i == 0)
    def _():
        h1_ref[...] = jnp.zeros_like(h1_ref)
        hb_ref[...] = jnp.zeros_like(hb_ref)
        xb0 = x_ref[0].astype(BF16)
        project(xb0, F0)
        project(xb0, Q0)

    @pl.when(lax.rem(i, tiles_per_seq) == 0)
    def _():
        st_ref[...] = jnp.zeros_like(st_ref)
        zbuf_ref[0:CARRY_ROWS, :] = jnp.zeros((CARRY_ROWS, CONV_WIDTH), F32)

    x = x_ref[0]
    xb = x.astype(BF16)

    lbl = lbl_ref[...]
    e = jnp.exp(lbl - jnp.max(lbl, axis=0, keepdims=True))
    lb = e[0:1, :] / jnp.sum(e, axis=0, keepdims=True)

    row = lax.broadcasted_iota(jnp.int32, (CHUNK, CHUNK), 0)
    col = lax.broadcasted_iota(jnp.int32, (CHUNK, CHUNK), 1)
    causal = row >= col
    tril = causal.astype(BF16)
    tril3 = jnp.concatenate([tril, tril, tril], axis=1)
    gnw = gnw_ref[...]
    cw = cw_ref[...]

    k_all, b_all = {}, {}
    scores_all, upd_all, qi_all, vb_all, decay_all = {}, {}, {}, {}, {}
    state = [st_ref[h] for h in range(HEADS)]
    hidden = {}

    def cumsum_level(c):
        f = lb + (1.0 - lb) * jax.nn.sigmoid(pf_ref[_chunk_rows(c), :])
        g = jnp.log(f)
        k_all[c] = 1.0 - f
        g3 = jnp.concatenate(_split3_bf16(g), axis=0)
        b_all[c] = jnp.dot(tril3, g3, preferred_element_type=F32)

    def scores_level(c):
        b, k = b_all[c], k_all[c]
        q = pq_ref[_chunk_rows(c), :]
        b_mid = b[CHUNK // 2:CHUNK // 2 + 1, :]
        b_last = b[CHUNK - 1:CHUNK, :]
        qd = (q * jnp.exp(b - b_mid)).astype(BF16)
        kd = (k * jnp.exp(b_mid - b)).astype(BF16)
        ks = (k * jnp.exp(b_last - b)).astype(BF16)
        qi_all[c] = (q * jnp.exp(b)).astype(BF16)
        decay_all[c] = jnp.exp(b_last)
        vb = pi_ref[_chunk_rows(c), :].astype(BF16)
        vb_all[c] = vb
        for h in range(HEADS):
            cs = _head_cols(h)
            scores_all[c, h] = lax.dot_general(qd[:, cs], kd[:, cs], _NT,
                                               preferred_element_type=F32)
            upd_all[c, h] = lax.dot_general(vb[:, cs], ks[:, cs], _TN,
                                            preferred_element_type=F32)

    def output_level(c):
        for h in range(HEADS):
            cs = _head_cols(h)
            scores = jnp.where(causal, scores_all[c, h], 0.0).astype(BF16)
            o_h = (jnp.dot(scores, vb_all[c][:, cs], preferred_element_type=F32)
                   + lax.dot_general(qi_all[c][:, cs], state[h].astype(BF16), _NT,
                                     preferred_element_type=F32))
            state[h] = state[h] * decay_all[c][:, cs] + upd_all[c, h]
            gate = pg_ref[_chunk_rows(c), cs]
            o_h = o_h * lax.rsqrt(jnp.mean(o_h * o_h, axis=-1, keepdims=True) + EPS)
            o_h = o_h * gnw[:, cs] * (gate * jax.nn.sigmoid(gate))
            oh_ref[_chunk_rows(c), cs] = o_h.astype(BF16)

    def conv():
        z0 = CARRY_ROWS
        zbuf_ref[z0:z0 + TIME_TILE, :] = pc_ref[...] * pu_ref[...]
        acc = (cw[0:1, :] * zbuf_ref[z0 - 2:z0 - 2 + TIME_TILE, :]
               + cw[1:2, :] * zbuf_ref[z0 - 1:z0 - 1 + TIME_TILE, :]
               + cw[2:3, :] * zbuf_ref[z0:z0 + TIME_TILE, :])
        y_ref[...] = (pb_ref[...] * acc).astype(BF16)
        zbuf_ref[0:CARRY_ROWS, :] = zbuf_ref[TIME_TILE:TIME_TILE + CARRY_ROWS, :]

    def mixer_finish():
        mix = (jnp.dot(oh_ref[...], w_out_ref[0:HGRN_WIDTH, :], preferred_element_type=F32)
               + jnp.dot(y_ref[...], w_out_ref[HGRN_WIDTH:, :], preferred_element_type=F32))
        return _layer_norm(ALPHA * x + mix, g1_ref[...], b1_ref[...])

    def ff1(j):
        a = jnp.dot(hb_ref[...], w1_ref[:, j * FF_BLOCK:(j + 1) * FF_BLOCK],
                    preferred_element_type=F32)
        a = jnp.maximum(a, 0.0)
        hidden[j] = (a * a).astype(BF16)

    def ff2(j):
        part = jnp.dot(hidden[j], w2_ref[j * FF_BLOCK:(j + 1) * FF_BLOCK, :],
                       preferred_element_type=F32)
        if j == 0:
            acc_ref[...] = part
        else:
            acc_ref[...] = acc_ref[...] + part

    def ff2_last_and_finish():
        j = FF_BLOCKS - 1
        part = jnp.dot(hidden[j], w2_ref[j * FF_BLOCK:(j + 1) * FF_BLOCK, :],
                       preferred_element_type=F32)
        o_ref[0] = _layer_norm(ALPHA * h1_ref[...] + (acc_ref[...] + part),
                               g2_ref[...], b2_ref[...])

    ff1(0)
    for c in range(CHUNKS_PER_TILE):
        cumsum_level(c)
    project(xb, I0)
    ff2(0)
    group = CHUNKS_PER_TILE // 2
    for c in range(group):
        scores_level(c)
    project(xb, G0)
    for c in range(group, CHUNKS_PER_TILE):
        scores_level(c)
    ff1(1)
    for c in range(group):
        output_level(c)
    project(xb, B0)
    for c in range(group, CHUNKS_PER_TILE):
        output_level(c)
    for h in range(HEADS):
        st_ref[h] = state[h]
    ff2(1)
    project(xb, C0)
    project(xb, U0)
    ff1(2)
    conv()
    ff1(3)
    h1_new = mixer_finish()
    ff2(2)
    ff2_last_and_finish()
    xnb = xn_ref[0].astype(BF16)
    project(xnb, F0)
    project(xnb, Q0)
    h1_ref[...] = h1_new
    hb_ref[...] = h1_new.astype(BF16)


def _resident(shape):
    return pl.BlockSpec(shape, lambda *_: (0,) * len(shape), pipeline_mode=pl.Buffered(1))


def kernel(x, w_in, lb_logits, gate_norm_w, conv_w, w_out, ln1_g, ln1_b, w_ff1, w_ff2, ln2_g, ln2_b):
    bsz, seq, d = x.shape
    assert d == D_MODEL and seq % TIME_TILE == 0 and CHUNKS_PER_TILE % 2 == 0
    tiles_per_seq = seq // TIME_TILE
    n_tiles = bsz * tiles_per_seq

    def tile_index(j):
        return (j // tiles_per_seq, j % tiles_per_seq, 0)

    return pl.pallas_call(
        functools.partial(_layer_kernel, tiles_per_seq=tiles_per_seq),
        grid=(n_tiles + 1,),
        in_specs=[
            pl.BlockSpec((1, TIME_TILE, D_MODEL), lambda i: tile_index(jnp.minimum(i, n_tiles - 1))),
            pl.BlockSpec((1, TIME_TILE, D_MODEL), lambda i: tile_index(jnp.minimum(i + 1, n_tiles - 1))),
            _resident((D_MODEL, IN_COLS)),
            _resident((DEPTH + 1, HGRN_WIDTH)),
            _resident((1, HGRN_WIDTH)),
            _resident((CONV_K, CONV_WIDTH)),
            _resident((D_MODEL, D_MODEL)),
            _resident((1, D_MODEL)),
            _resident((1, D_MODEL)),
            _resident((D_MODEL, D_FF)),
            _resident((D_FF, D_MODEL)),
            _resident((1, D_MODEL)),
            _resident((1, D_MODEL)),
        ],
        out_specs=pl.BlockSpec((1, TIME_TILE, D_MODEL), lambda i: tile_index(jnp.maximum(i - 1, 0))),
        out_shape=jax.ShapeDtypeStruct(x.shape, F32),
        scratch_shapes=[
            *[pltpu.VMEM((TIME_TILE, SECTION), F32)] * 7,
            pltpu.VMEM((HEADS, HEAD_DIM, HEAD_DIM), F32),
            pltpu.VMEM((CARRY_ROWS + TIME_TILE, CONV_WIDTH), F32),
            pltpu.VMEM((TIME_TILE, HGRN_WIDTH), BF16),
            pltpu.VMEM((TIME_TILE, CONV_WIDTH), BF16),
            pltpu.VMEM((TIME_TILE, D_MODEL), F32),
            pltpu.VMEM((TIME_TILE, D_MODEL), BF16),
            pltpu.VMEM((TIME_TILE, D_MODEL), F32),
        ],
        compiler_params=pltpu.CompilerParams(
            dimension_semantics=("arbitrary",),
            vmem_limit_bytes=VMEM_LIMIT_BYTES),
        name="hgrn2_conv_mlp_layer",
    )(x, x, w_in[0].astype(BF16), lb_logits, gate_norm_w, conv_w[0], w_out[0].astype(BF16),
      ln1_g, ln1_b, w_ff1[0].astype(BF16), w_ff2[0].astype(BF16), ln2_g, ln2_b)
```

```python
import functools

import jax
import jax.numpy as jnp
from jax import lax
from jax.experimental import pallas as pl
from jax.experimental.pallas import tpu as pltpu

D_MODEL = 1024
HGRN_WIDTH = 512
CONV_WIDTH = 512
HEAD_DIM = 128
HEADS = HGRN_WIDTH // HEAD_DIM
CHUNK = 64
CONV_K = 3
D_FF = 4 * D_MODEL
IN_COLS = 4 * HGRN_WIDTH + 3 * CONV_WIDTH
DEPTH = 1
ALPHA = (2 * DEPTH) ** 0.25
EPS = 1e-5

Q0, F0, I0, G0, B0, C0, U0 = (0, 512, 1024, 1536, 2048, 2560, 3072)
SECTION = 512

TIME_TILE = 512
CHUNKS_PER_TILE = TIME_TILE // CHUNK
FF_BLOCK = 1024
FF_BLOCKS = D_FF // FF_BLOCK
WEIGHT_CHUNK_ROWS = 64
CARRY_ROWS = 8
VMEM_LIMIT_BYTES = 60 * 1024 * 1024

F32 = jnp.float32
BF16 = jnp.bfloat16

_NT = (((1,), (1,)), ((), ()))
_TN = (((0,), (0,)), ((), ()))


def _layer_norm(x, g, b):
    mu = jnp.mean(x, axis=-1, keepdims=True)
    xc = x - mu
    var = jnp.mean(xc * xc, axis=-1, keepdims=True)
    return xc * lax.rsqrt(var + EPS) * g + b


def _split3_bf16(a):
    hi = a.astype(BF16)
    r1 = a - hi.astype(F32)
    mid = r1.astype(BF16)
    lo = (r1 - mid.astype(F32)).astype(BF16)
    return hi, mid, lo


def _chunk_rows(c):
    return slice(c * CHUNK, (c + 1) * CHUNK)


def _head_cols(h):
    return slice(h * HEAD_DIM, (h + 1) * HEAD_DIM)


def _load_weight_as_bf16(src_hbm, dst_ref, stage_ref, sem_ref):
    rows, cols = src_hbm.shape
    chunk = stage_ref.shape[1]
    n_chunks = rows // chunk

    def copy(k, slot):
        return pltpu.make_async_copy(src_hbm.at[pl.ds(k * chunk, chunk), :],
                                     stage_ref.at[slot, :, pl.ds(0, cols)],
                                     sem_ref.at[slot])

    copy(0, 0).start()

    def body(k, carry):
        slot = lax.rem(k, 2)

        @pl.when(k + 1 < n_chunks)
        def _():
            copy(k + 1, 1 - slot).start()

        copy(k, slot).wait()
        r0 = pl.multiple_of(k * chunk, chunk)
        dst_ref[pl.ds(r0, chunk), :] = stage_ref[slot, :, 0:cols].astype(BF16)
        return carry

    lax.fori_loop(0, n_chunks, body, 0)


def _layer_kernel(x_ref, xn_ref, w_in_hbm, lbl_ref, gnw_ref, cw_ref, w_out_hbm, g1_ref, b1_ref,
                  w1_hbm, w2_hbm, g2_ref, b2_ref,
                  o_ref,
                  w_in_ref, w_out_ref, w1_ref, w2_ref, stage_ref, stage_sem,
                  pq_ref, pf_ref, pi_ref, pg_ref, pb_ref, pc_ref, pu_ref,
                  st_ref, zbuf_ref, oh_ref, y_ref, h1_ref, hb_ref, acc_ref,
                  *, tiles_per_seq):
    i = pl.program_id(0)
    sec = {Q0: pq_ref, F0: pf_ref, I0: pi_ref, G0: pg_ref, B0: pb_ref, C0: pc_ref, U0: pu_ref}

    def project(xb, c0):
        sec[c0][...] = jnp.dot(xb, w_in_ref[:, c0:c0 + SECTION], preferred_element_type=F32)

    @pl.when(i == 0)
    def _():
        _load_weight_as_bf16(w_in_hbm, w_in_ref, stage_ref, stage_sem)
        _load_weight_as_bf16(w_out_hbm, w_out_ref, stage_ref, stage_sem)
        _load_weight_as_bf16(w1_hbm, w1_ref, stage_ref, stage_sem)
        _load_weight_as_bf16(w2_hbm, w2_ref, stage_ref, stage_sem)
        h1_ref[...] = jnp.zeros_like(h1_ref)
        hb_ref[...] = jnp.zeros_like(hb_ref)
        xb0 = x_ref[0].astype(BF16)
        project(xb0, F0)
        project(xb0, Q0)

    @pl.when(lax.rem(i, tiles_per_seq) == 0)
    def _():
        st_ref[...] = jnp.zeros_like(st_ref)
        zbuf_ref[0:CARRY_ROWS, :] = jnp.zeros((CARRY_ROWS, CONV_WIDTH), F32)

    x = x_ref[0]
    xb = x.astype(BF16)

    lbl = lbl_ref[...]
    e = jnp.exp(lbl - jnp.max(lbl, axis=0, keepdims=True))
    lb = e[0:1, :] / jnp.sum(e, axis=0, keepdims=True)

    row = lax.broadcasted_iota(jnp.int32, (CHUNK, CHUNK), 0)
    col = lax.broadcasted_iota(jnp.int32, (CHUNK, CHUNK), 1)
    causal = row >= col
    tril = causal.astype(BF16)
    tril3 = jnp.concatenate([tril, tril, tril], axis=1)
    gnw = gnw_ref[...]
    cw = cw_ref[...]

    k_all, b_all = {}, {}
    scores_all, upd_all, qi_all, vb_all, decay_all = {}, {}, {}, {}, {}
    state = [st_ref[h] for h in range(HEADS)]
    hidden = {}

    def cumsum_level(c):
        f = lb + (1.0 - lb) * jax.nn.sigmoid(pf_ref[_chunk_rows(c), :])
        g = jnp.log(f)
        k_all[c] = 1.0 - f
        g3 = jnp.concatenate(_split3_bf16(g), axis=0)
        b_all[c] = jnp.dot(tril3, g3, preferred_element_type=F32)

    def scores_level(c):
        b, k = b_all[c], k_all[c]
        q = pq_ref[_chunk_rows(c), :]
        b_mid = b[CHUNK // 2:CHUNK // 2 + 1, :]
        b_last = b[CHUNK - 1:CHUNK, :]
        qd = (q * jnp.exp(b - b_mid)).astype(BF16)
        kd = (k * jnp.exp(b_mid - b)).astype(BF16)
        ks = (k * jnp.exp(b_last - b)).astype(BF16)
        qi_all[c] = (q * jnp.exp(b)).astype(BF16)
        decay_all[c] = jnp.exp(b_last)
        vb = pi_ref[_chunk_rows(c), :].astype(BF16)
        vb_all[c] = vb
        for h in range(HEADS):
            cs = _head_cols(h)
            scores_all[c, h] = lax.dot_general(qd[:, cs], kd[:, cs], _NT,
                                               preferred_element_type=F32)
            upd_all[c, h] = lax.dot_general(vb[:, cs], ks[:, cs], _TN,
                                            preferred_element_type=F32)

    def output_level(c):
        for h in range(HEADS):
            cs = _head_cols(h)
            scores = jnp.where(causal, scores_all[c, h], 0.0).astype(BF16)
            o_h = (jnp.dot(scores, vb_all[c][:, cs], preferred_element_type=F32)
                   + lax.dot_general(qi_all[c][:, cs], state[h].astype(BF16), _NT,
                                     preferred_element_type=F32))
            state[h] = state[h] * decay_all[c][:, cs] + upd_all[c, h]
            gate = pg_ref[_chunk_rows(c), cs]
            o_h = o_h * lax.rsqrt(jnp.mean(o_h * o_h, axis=-1, keepdims=True) + EPS)
            o_h = o_h * gnw[:, cs] * (gate * jax.nn.sigmoid(gate))
            oh_ref[_chunk_rows(c), cs] = o_h.astype(BF16)

    def conv():
        z0 = CARRY_ROWS
        zbuf_ref[z0:z0 + TIME_TILE, :] = pc_ref[...] * pu_ref[...]
        acc = (cw[0:1, :] * zbuf_ref[z0 - 2:z0 - 2 + TIME_TILE, :]
               + cw[1:2, :] * zbuf_ref[z0 - 1:z0 - 1 + TIME_TILE, :]
               + cw[2:3, :] * zbuf_ref[z0:z0 + TIME_TILE, :])
        y_ref[...] = (pb_ref[...] * acc).astype(BF16)
        zbuf_ref[0:CARRY_ROWS, :] = zbuf_ref[TIME_TILE:TIME_TILE + CARRY_ROWS, :]

    def mixer_finish():
        mix = (jnp.dot(oh_ref[...], w_out_ref[0:HGRN_WIDTH, :], preferred_element_type=F32)
               + jnp.dot(y_ref[...], w_out_ref[HGRN_WIDTH:, :], preferred_element_type=F32))
        return _layer_norm(ALPHA * x + mix, g1_ref[...], b1_ref[...])

    def ff1(j):
        a = jnp.dot(hb_ref[...], w1_ref[:, j * FF_BLOCK:(j + 1) * FF_BLOCK],
                    preferred_element_type=F32)
        a = jnp.maximum(a, 0.0)
        hidden[j] = (a * a).astype(BF16)

    def ff2(j):
        part = jnp.dot(hidden[j], w2_ref[j * FF_BLOCK:(j + 1) * FF_BLOCK, :],
                       preferred_element_type=F32)
        if j == 0:
            acc_ref[...] = part
        else:
            acc_ref[...] = acc_ref[...] + part

    def ff2_last_and_finish():
        j = FF_BLOCKS - 1
        part = jnp.dot(hidden[j], w2_ref[j * FF_BLOCK:(j + 1) * FF_BLOCK, :],
                       preferred_element_type=F32)
        o_ref[0] = _layer_norm(ALPHA * h1_ref[...] + (acc_ref[...] + part),
                               g2_ref[...], b2_ref[...])

    ff1(0)
    for c in range(CHUNKS_PER_TILE):
        cumsum_level(c)
    project(xb, I0)
    ff2(0)
    group = CHUNKS_PER_TILE // 2
    for c in range(group):
        scores_level(c)
    project(xb, G0)
    for c in range(group, CHUNKS_PER_TILE):
        scores_level(c)
    ff1(1)
    for c in range(group):
        output_level(c)
    project(xb, B0)
    for c in range(group, CHUNKS_PER_TILE):
        output_level(c)
    for h in range(HEADS):
        st_ref[h] = state[h]
    ff2(1)
    project(xb, C0)
    project(xb, U0)
    ff1(2)
    conv()
    ff1(3)
    ff2(2)
    ff2_last_and_finish()
    h1_new = mixer_finish()
    xnb = xn_ref[0].astype(BF16)
    project(xnb, F0)
    project(xnb, Q0)
    h1_ref[...] = h1_new
    hb_ref[...] = h1_new.astype(BF16)


def _resident(shape):
    return pl.BlockSpec(shape, lambda *_: (0,) * len(shape), pipeline_mode=pl.Buffered(1))


def kernel(x, w_in, lb_logits, gate_norm_w, conv_w, w_out, ln1_g, ln1_b, w_ff1, w_ff2, ln2_g, ln2_b):
    bsz, seq, d = x.shape
    assert d == D_MODEL and seq % TIME_TILE == 0 and CHUNKS_PER_TILE % 2 == 0
    tiles_per_seq = seq // TIME_TILE
    n_tiles = bsz * tiles_per_seq

    def tile_index(j):
        return (j // tiles_per_seq, j % tiles_per_seq, 0)

    return pl.pallas_call(
        functools.partial(_layer_kernel, tiles_per_seq=tiles_per_seq),
        grid=(n_tiles + 1,),
        in_specs=[
            pl.BlockSpec((1, TIME_TILE, D_MODEL), lambda i: tile_index(jnp.minimum(i, n_tiles - 1))),
            pl.BlockSpec((1, TIME_TILE, D_MODEL), lambda i: tile_index(jnp.minimum(i + 1, n_tiles - 1))),
            pl.BlockSpec(memory_space=pl.ANY),
            _resident((DEPTH + 1, HGRN_WIDTH)),
            _resident((1, HGRN_WIDTH)),
            _resident((CONV_K, CONV_WIDTH)),
            pl.BlockSpec(memory_space=pl.ANY),
            _resident((1, D_MODEL)),
            _resident((1, D_MODEL)),
            pl.BlockSpec(memory_space=pl.ANY),
            pl.BlockSpec(memory_space=pl.ANY),
            _resident((1, D_MODEL)),
            _resident((1, D_MODEL)),
        ],
        out_specs=pl.BlockSpec((1, TIME_TILE, D_MODEL), lambda i: tile_index(jnp.maximum(i - 1, 0))),
        out_shape=jax.ShapeDtypeStruct(x.shape, F32),
        scratch_shapes=[
            pltpu.VMEM((D_MODEL, IN_COLS), BF16),
            pltpu.VMEM((D_MODEL, D_MODEL), BF16),
            pltpu.VMEM((D_MODEL, D_FF), BF16),
            pltpu.VMEM((D_FF, D_MODEL), BF16),
            pltpu.VMEM((2, WEIGHT_CHUNK_ROWS, D_FF), F32),
            pltpu.SemaphoreType.DMA((2,)),
            *[pltpu.VMEM((TIME_TILE, SECTION), F32)] * 7,
            pltpu.VMEM((HEADS, HEAD_DIM, HEAD_DIM), F32),
            pltpu.VMEM((CARRY_ROWS + TIME_TILE, CONV_WIDTH), F32),
            pltpu.VMEM((TIME_TILE, HGRN_WIDTH), BF16),
            pltpu.VMEM((TIME_TILE, CONV_WIDTH), BF16),
            pltpu.VMEM((TIME_TILE, D_MODEL), F32),
            pltpu.VMEM((TIME_TILE, D_MODEL), BF16),
            pltpu.VMEM((TIME_TILE, D_MODEL), F32),
        ],
        compiler_params=pltpu.CompilerParams(
            dimension_semantics=("arbitrary",),
            vmem_limit_bytes=VMEM_LIMIT_BYTES),
        name="hgrn2_conv_mlp_layer",
    )(x, x, w_in[0], lb_logits, gate_norm_w, conv_w[0], w_out[0],
      ln1_g, ln1_b, w_ff1[0], w_ff2[0], ln2_g, ln2_b)
```

```python
import functools

import jax
import jax.numpy as jnp
from jax import lax
from jax.experimental import pallas as pl
from jax.experimental.pallas import tpu as pltpu

D_MODEL = 1024
HGRN_WIDTH = 512
CONV_WIDTH = 512
HEAD_DIM = 128
HEADS = HGRN_WIDTH // HEAD_DIM
CHUNK = 64
CONV_K = 3
D_FF = 4 * D_MODEL
IN_COLS = 4 * HGRN_WIDTH + 3 * CONV_WIDTH
DEPTH = 1
ALPHA = (2 * DEPTH) ** 0.25
EPS = 1e-5

Q0, F0, I0, G0, B0, C0, U0 = (0, 512, 1024, 1536, 2048, 2560, 3072)
SECTION = 512

TIME_TILE = 512
CHUNKS_PER_TILE = TIME_TILE // CHUNK
FF_BLOCK = 1024
FF_BLOCKS = D_FF // FF_BLOCK
WEIGHT_BLOCK_ROWS = 256
WEIGHT_BLOCK_COLS = D_MODEL
CARRY_ROWS = 8
VMEM_LIMIT_BYTES = 60 * 1024 * 1024

F32 = jnp.float32
BF16 = jnp.bfloat16

_NT = (((1,), (1,)), ((), ()))
_TN = (((0,), (0,)), ((), ()))


def _layer_norm(x, g, b):
    mu = jnp.mean(x, axis=-1, keepdims=True)
    xc = x - mu
    var = jnp.mean(xc * xc, axis=-1, keepdims=True)
    return xc * lax.rsqrt(var + EPS) * g + b


def _split3_bf16(a):
    hi = a.astype(BF16)
    r1 = a - hi.astype(F32)
    mid = r1.astype(BF16)
    lo = (r1 - mid.astype(F32)).astype(BF16)
    return hi, mid, lo


def _chunk_rows(c):
    return slice(c * CHUNK, (c + 1) * CHUNK)


def _head_cols(h):
    return slice(h * HEAD_DIM, (h + 1) * HEAD_DIM)


def _load_weights_as_bf16(pairs, slots, sem_ref):
    blocks = []
    for src, dst in pairs:
        rows, cols = src.shape
        for r0 in range(0, rows, WEIGHT_BLOCK_ROWS):
            for c0 in range(0, cols, WEIGHT_BLOCK_COLS):
                blocks.append((src, dst, r0, c0, min(WEIGHT_BLOCK_COLS, cols - c0)))

    def copy(n):
        src, _, r0, c0, width = blocks[n]
        slot = n % len(slots)
        return pltpu.make_async_copy(src.at[r0:r0 + WEIGHT_BLOCK_ROWS, c0:c0 + width],
                                     slots[slot].at[:, 0:width], sem_ref.at[slot])

    for n in range(min(len(slots), len(blocks))):
        copy(n).start()
    for n, (_, dst, r0, c0, width) in enumerate(blocks):
        copy(n).wait()
        dst[r0:r0 + WEIGHT_BLOCK_ROWS, c0:c0 + width] = slots[n % len(slots)][:, 0:width].astype(BF16)
        if n + len(slots) < len(blocks):
            copy(n + len(slots)).start()


def _layer_kernel(x_ref, xn_ref, w_in_hbm, lbl_ref, gnw_ref, cw_ref, w_out_hbm, g1_ref, b1_ref,
                  w1_hbm, w2_hbm, g2_ref, b2_ref,
                  o_ref,
                  w_in_ref, w_out_ref, w1_ref, w2_ref, stage_sem,
                  pq_ref, pf_ref, pi_ref, pg_ref, pb_ref, pc_ref, pu_ref,
                  st_ref, zbuf_ref, oh_ref, y_ref, h1_ref, hb_ref, acc_ref,
                  *, tiles_per_seq):
    i = pl.program_id(0)
    sec = {Q0: pq_ref, F0: pf_ref, I0: pi_ref, G0: pg_ref, B0: pb_ref, C0: pc_ref, U0: pu_ref}

    def project(xb, c0):
        sec[c0][...] = jnp.dot(xb, w_in_ref[:, c0:c0 + SECTION], preferred_element_type=F32)

    @pl.when(i == 0)
    def _():
        slots = [buf.at[r0:r0 + WEIGHT_BLOCK_ROWS, :] for buf in (acc_ref, h1_ref)
                 for r0 in range(0, TIME_TILE, WEIGHT_BLOCK_ROWS)]
        _load_weights_as_bf16([(w_in_hbm, w_in_ref), (w_out_hbm, w_out_ref),
                               (w1_hbm, w1_ref), (w2_hbm, w2_ref)], slots, stage_sem)
        h1_ref[...] = jnp.zeros_like(h1_ref)
        hb_ref[...] = jnp.zeros_like(hb_ref)
        xb0 = x_ref[0].astype(BF16)
        project(xb0, F0)
        project(xb0, Q0)

    @pl.when(lax.rem(i, tiles_per_seq) == 0)
    def _():
        st_ref[...] = jnp.zeros_like(st_ref)
        zbuf_ref[0:CARRY_ROWS, :] = jnp.zeros((CARRY_ROWS, CONV_WIDTH), F32)

    x = x_ref[0]
    xb = x.astype(BF16)

    lbl = lbl_ref[...]
    e = jnp.exp(lbl - jnp.max(lbl, axis=0, keepdims=True))
    lb = e[0:1, :] / jnp.sum(e, axis=0, keepdims=True)

    row = lax.broadcasted_iota(jnp.int32, (CHUNK, CHUNK), 0)
    col = lax.broadcasted_iota(jnp.int32, (CHUNK, CHUNK), 1)
    causal = row >= col
    tril = causal.astype(BF16)
    tril3 = jnp.concatenate([tril, tril, tril], axis=1)
    gnw = gnw_ref[...]
    cw = cw_ref[...]

    k_all, b_all = {}, {}
    scores_all, upd_all, qi_all, vb_all, decay_all = {}, {}, {}, {}, {}
    state = [st_ref[h] for h in range(HEADS)]
    hidden = {}

    def cumsum_level(c):
        f = lb + (1.0 - lb) * jax.nn.sigmoid(pf_ref[_chunk_rows(c), :])
        g = jnp.log(f)
        k_all[c] = 1.0 - f
        g3 = jnp.concatenate(_split3_bf16(g), axis=0)
        b_all[c] = jnp.dot(tril3, g3, preferred_element_type=F32)

    def scores_level(c):
        b, k = b_all[c], k_all[c]
        q = pq_ref[_chunk_rows(c), :]
        b_mid = b[CHUNK // 2:CHUNK // 2 + 1, :]
        b_last = b[CHUNK - 1:CHUNK, :]
        qd = (q * jnp.exp(b - b_mid)).astype(BF16)
        kd = (k * jnp.exp(b_mid - b)).astype(BF16)
        ks = (k * jnp.exp(b_last - b)).astype(BF16)
        qi_all[c] = (q * jnp.exp(b)).astype(BF16)
        decay_all[c] = jnp.exp(b_last)
        vb = pi_ref[_chunk_rows(c), :].astype(BF16)
        vb_all[c] = vb
        for h in range(HEADS):
            cs = _head_cols(h)
            scores_all[c, h] = lax.dot_general(qd[:, cs], kd[:, cs], _NT,
                                               preferred_element_type=F32)
            upd_all[c, h] = lax.dot_general(vb[:, cs], ks[:, cs], _TN,
                                            preferred_element_type=F32)

    def output_level(c):
        for h in range(HEADS):
            cs = _head_cols(h)
            scores = jnp.where(causal, scores_all[c, h], 0.0).astype(BF16)
            o_h = (jnp.dot(scores, vb_all[c][:, cs], preferred_element_type=F32)
                   + lax.dot_general(qi_all[c][:, cs], state[h].astype(BF16), _NT,
                                     preferred_element_type=F32))
            state[h] = state[h] * decay_all[c][:, cs] + upd_all[c, h]
            gate = pg_ref[_chunk_rows(c), cs]
            o_h = o_h * lax.rsqrt(jnp.mean(o_h * o_h, axis=-1, keepdims=True) + EPS)
            o_h = o_h * gnw[:, cs] * (gate * jax.nn.sigmoid(gate))
            oh_ref[_chunk_rows(c), cs] = o_h.astype(BF16)

    def conv():
        z0 = CARRY_ROWS
        zbuf_ref[z0:z0 + TIME_TILE, :] = pc_ref[...] * pu_ref[...]
        acc = (cw[0:1, :] * zbuf_ref[z0 - 2:z0 - 2 + TIME_TILE, :]
               + cw[1:2, :] * zbuf_ref[z0 - 1:z0 - 1 + TIME_TILE, :]
               + cw[2:3, :] * zbuf_ref[z0:z0 + TIME_TILE, :])
        y_ref[...] = (pb_ref[...] * acc).astype(BF16)
        zbuf_ref[0:CARRY_ROWS, :] = zbuf_ref[TIME_TILE:TIME_TILE + CARRY_ROWS, :]

    def mixer_finish():
        mix = (jnp.dot(oh_ref[...], w_out_ref[0:HGRN_WIDTH, :], preferred_element_type=F32)
               + jnp.dot(y_ref[...], w_out_ref[HGRN_WIDTH:, :], preferred_element_type=F32))
        return _layer_norm(ALPHA * x + mix, g1_ref[...], b1_ref[...])

    def ff1(j):
        a = jnp.dot(hb_ref[...], w1_ref[:, j * FF_BLOCK:(j + 1) * FF_BLOCK],
                    preferred_element_type=F32)
        a = jnp.maximum(a, 0.0)
        hidden[j] = (a * a).astype(BF16)

    def ff2(j):
        part = jnp.dot(hidden[j], w2_ref[j * FF_BLOCK:(j + 1) * FF_BLOCK, :],
                       preferred_element_type=F32)
        if j == 0:
            acc_ref[...] = part
        else:
            acc_ref[...] = acc_ref[...] + part

    def ff2_last_and_finish():
        j = FF_BLOCKS - 1
        part = jnp.dot(hidden[j], w2_ref[j * FF_BLOCK:(j + 1) * FF_BLOCK, :],
                       preferred_element_type=F32)
        o_ref[0] = _layer_norm(ALPHA * h1_ref[...] + (acc_ref[...] + part),
                               g2_ref[...], b2_ref[...])

    ff1(0)
    for c in range(CHUNKS_PER_TILE):
        cumsum_level(c)
    project(xb, I0)
    ff2(0)
    group = CHUNKS_PER_TILE // 2
    for c in range(group):
        scores_level(c)
    project(xb, G0)
    for c in range(group, CHUNKS_PER_TILE):
        scores_level(c)
    ff1(1)
    for c in range(group):
        output_level(c)
    project(xb, B0)
    for c in range(group, CHUNKS_PER_TILE):
        output_level(c)
    for h in range(HEADS):
        st_ref[h] = state[h]
    ff2(1)
    project(xb, C0)
    project(xb, U0)
    ff1(2)
    conv()
    ff1(3)
    ff2(2)
    ff2_last_and_finish()
    h1_new = mixer_finish()
    xnb = xn_ref[0].astype(BF16)
    project(xnb, F0)
    project(xnb, Q0)
    h1_ref[...] = h1_new
    hb_ref[...] = h1_new.astype(BF16)


def _resident(shape):
    return pl.BlockSpec(shape, lambda *_: (0,) * len(shape), pipeline_mode=pl.Buffered(1))


def kernel(x, w_in, lb_logits, gate_norm_w, conv_w, w_out, ln1_g, ln1_b, w_ff1, w_ff2, ln2_g, ln2_b):
    bsz, seq, d = x.shape
    assert d == D_MODEL and seq % TIME_TILE == 0 and CHUNKS_PER_TILE % 2 == 0
    tiles_per_seq = seq // TIME_TILE
    n_tiles = bsz * tiles_per_seq

    def tile_index(j):
        return (j // tiles_per_seq, j % tiles_per_seq, 0)

    return pl.pallas_call(
        functools.partial(_layer_kernel, tiles_per_seq=tiles_per_seq),
        grid=(n_tiles + 1,),
        in_specs=[
            pl.BlockSpec((1, TIME_TILE, D_MODEL), lambda i: tile_index(jnp.minimum(i, n_tiles - 1))),
            pl.BlockSpec((1, TIME_TILE, D_MODEL), lambda i: tile_index(jnp.minimum(i + 1, n_tiles - 1))),
            pl.BlockSpec(memory_space=pl.ANY),
            _resident((DEPTH + 1, HGRN_WIDTH)),
            _resident((1, HGRN_WIDTH)),
            _resident((CONV_K, CONV_WIDTH)),
            pl.BlockSpec(memory_space=pl.ANY),
            _resident((1, D_MODEL)),
            _resident((1, D_MODEL)),
            pl.BlockSpec(memory_space=pl.ANY),
            pl.BlockSpec(memory_space=pl.ANY),
            _resident((1, D_MODEL)),
            _resident((1, D_MODEL)),
        ],
        out_specs=pl.BlockSpec((1, TIME_TILE, D_MODEL), lambda i: tile_index(jnp.maximum(i - 1, 0))),
        out_shape=jax.ShapeDtypeStruct(x.shape, F32),
        scratch_shapes=[
            pltpu.VMEM((D_MODEL, IN_COLS), BF16),
            pltpu.VMEM((D_MODEL, D_MODEL), BF16),
            pltpu.VMEM((D_MODEL, D_FF), BF16),
            pltpu.VMEM((D_FF, D_MODEL), BF16),
            pltpu.SemaphoreType.DMA((2 * TIME_TILE // WEIGHT_BLOCK_ROWS,)),
            *[pltpu.VMEM((TIME_TILE, SECTION), F32)] * 7,
            pltpu.VMEM((HEADS, HEAD_DIM, HEAD_DIM), F32),
            pltpu.VMEM((CARRY_ROWS + TIME_TILE, CONV_WIDTH), F32),
            pltpu.VMEM((TIME_TILE, HGRN_WIDTH), BF16),
            pltpu.VMEM((TIME_TILE, CONV_WIDTH), BF16),
            pltpu.VMEM((TIME_TILE, D_MODEL), F32),
            pltpu.VMEM((TIME_TILE, D_MODEL), BF16),
            pltpu.VMEM((TIME_TILE, D_MODEL), F32),
        ],
        compiler_params=pltpu.CompilerParams(
            dimension_semantics=("arbitrary",),
            vmem_limit_bytes=VMEM_LIMIT_BYTES),
        name="hgrn2_conv_mlp_layer",
    )(x, x, w_in[0], lb_logits, gate_norm_w, conv_w[0], w_out[0],
      ln1_g, ln1_b, w_ff1[0], w_ff2[0], ln2_g, ln2_b)
```
